```python
import math
import jax, jax.numpy as jnp
from jax import lax
import numpy as np

D_MODEL = 1024
BATCH = 16
SEQ = 256
DEPTH = 2
DEC_BATCH = 4
DEC_SEQ = 4096
PAST_LEN = 256

GRID_W = 64
D_MIX = 1024
EPS = 1e-6
MLA_HEADS = 4
Q_LORA = 256
KV_LORA = 128
QK_NOPE = 64
QK_ROPE = 32
V_HEAD = 64
MLA_OUT = MLA_HEADS * V_HEAD
ROPE_THETA = 10000.0
ROPE_FREQS = QK_ROPE // 4
MLA_SCALE = (QK_NOPE + QK_ROPE) ** -0.5
ATTN_BLOCK = 128
SSD_HEADS = 8
SSD_HEADDIM = 64
SSD_INNER = SSD_HEADS * SSD_HEADDIM
SSD_GROUPS = 2
SSD_DSTATE = 64
SSD_CONV = 5
SSD_CHUNK = 128
SSD_CONV_DIM = SSD_INNER + 2 * SSD_GROUPS * SSD_DSTATE
CM_CH = 256
CM_KERNEL = 31
PEER_HEADS = 8
PEER_NKEYS = 128
PEER_EXPERTS = PEER_NKEYS * PEER_NKEYS
PEER_QDIM = 256
PEER_TOPK = 16
PEER_BLOCK = 128
IN_OFFSETS = (Q_LORA, Q_LORA + KV_LORA, Q_LORA + KV_LORA + QK_ROPE,
              Q_LORA + KV_LORA + QK_ROPE + SSD_INNER,
              Q_LORA + KV_LORA + QK_ROPE + SSD_INNER + SSD_CONV_DIM,
              Q_LORA + KV_LORA + QK_ROPE + SSD_INNER + SSD_CONV_DIM + 2 * SSD_HEADS)
IN_DIM = Q_LORA + KV_LORA + QK_ROPE + SSD_INNER + SSD_CONV_DIM + 2 * SSD_HEADS + 2 * CM_CH

kernel_name = 'hybrid_mla_ssd_conformer_peer_diffusion_step'


def rmsnorm(x, g):
    xf = x.astype(jnp.float32)
    y = xf * lax.rsqrt(jnp.mean(xf * xf, axis=-1, keepdims=True) + EPS)
    return (y * g.astype(jnp.float32)).astype(x.dtype)


def layernorm(x, g, b):
    xf = x.astype(jnp.float32)
    mu = jnp.mean(xf, axis=-1, keepdims=True)
    var = jnp.mean(jnp.square(xf - mu), axis=-1, keepdims=True)
    y = (xf - mu) * lax.rsqrt(var + EPS) * g.astype(jnp.float32) + b.astype(jnp.float32)
    return y.astype(x.dtype)


def axial_rope_tables(n_tokens):
    rows = n_tokens // GRID_W
    row = jnp.repeat(jnp.arange(rows, dtype=jnp.float32), GRID_W)
    col = jnp.tile(jnp.arange(GRID_W, dtype=jnp.float32), rows)
    inv = ROPE_THETA ** (-jnp.arange(ROPE_FREQS, dtype=jnp.float32) / ROPE_FREQS)
    ang = jnp.stack([row[:, None] * inv, col[:, None] * inv], axis=1)
    return jnp.cos(ang), jnp.sin(ang)


def apply_axial_rope(x, cos, sin):
    xs = x.reshape(x.shape[:-1] + (2, 2, ROPE_FREQS))
    x1, x2 = xs[..., 0, :], xs[..., 1, :]
    out = jnp.stack([x1 * cos - x2 * sin, x2 * cos + x1 * sin], axis=-2)
    return out.reshape(x.shape).astype(x.dtype)


def depthwise_conv(x, w, b):
    k, ch = w.shape
    y = lax.conv_general_dilated(x, w[:, None, :].astype(x.dtype), window_strides=(1,),
                                 padding=((k // 2, k // 2),), dimension_numbers=('NWC', 'WIO', 'NWC'),
                                 feature_group_count=ch)
    return y + b.astype(x.dtype)


def blocked_attention(q, k, v, scale):
    bsz, tq, nh, dq = q.shape
    nb = tq // ATTN_BLOCK
    qb = q.reshape(bsz, nb, ATTN_BLOCK, nh, dq).transpose(1, 0, 2, 3, 4)

    def one(qblk):
        s = jnp.einsum('bqhd,bkhd->bhqk', qblk, k).astype(jnp.float32) * scale
        p = jax.nn.softmax(s, axis=-1).astype(v.dtype)
        return jnp.einsum('bhqk,bkhd->bqhd', p, v)

    o = lax.map(one, qb)
    return o.transpose(1, 0, 2, 3, 4).reshape(bsz, tq, nh, v.shape[-1])


def mla_keys(ckv, kpe, w_ukv):
    bsz, t, _ = ckv.shape
    kv = jnp.einsum('btr,rf->btf', ckv, w_ukv).reshape(bsz, t, MLA_HEADS, QK_NOPE + V_HEAD)
    k_pe_h = jnp.broadcast_to(kpe[:, :, None, :], (bsz, t, MLA_HEADS, QK_ROPE)).astype(kv.dtype)
    return jnp.concatenate([kv[..., :QK_NOPE], k_pe_h], axis=-1), kv[..., QK_NOPE:]


def segsum(a):
    t = a.shape[-1]
    cs = jnp.cumsum(a, axis=-1)
    diff = cs[..., :, None] - cs[..., None, :]
    mask = jnp.tril(jnp.ones((t, t), dtype=bool))
    return jnp.where(mask, diff, -jnp.inf)


def ssd_scan(x, dt, a_neg, bm, cm, h0):
    b, l, h, p = x.shape
    g, n = bm.shape[-2:]
    nc, lc = l // SSD_CHUNK, SSD_CHUNK
    rep = h // g
    bh = jnp.repeat(bm.astype(jnp.float32), rep, axis=2).reshape(b, nc, lc, h, n)
    ch = jnp.repeat(cm.astype(jnp.float32), rep, axis=2).reshape(b, nc, lc, h, n)
    xdt = (x.astype(jnp.float32) * dt[..., None]).reshape(b, nc, lc, h, p)
    a = (dt * a_neg).reshape(b, nc, lc, h).transpose(0, 3, 1, 2)
    a_cum = jnp.cumsum(a, axis=-1)
    lmat = jnp.exp(segsum(a))
    scores = jnp.einsum('bclhn,bcshn->bhcls', ch, bh) * lmat
    y_diag = jnp.einsum('bhcls,bcshp->bclhp', scores, xdt)
    decay_states = jnp.exp(a_cum[..., -1:] - a_cum)
    states = jnp.einsum('bclhn,bhcl,bclhp->bchpn', bh, decay_states, xdt)
    states = jnp.concatenate([h0[:, None].astype(jnp.float32), states], axis=1)
    chunk_a = jnp.pad(a_cum[..., -1], ((0, 0), (0, 0), (1, 0)))
    decay_chunk = jnp.exp(segsum(chunk_a))
    new_states = jnp.einsum('bhzc,bchpn->bzhpn', decay_chunk, states)
    states, final = new_states[:, :-1], new_states[:, -1]
    y_off = jnp.einsum('bclhn,bchpn,bhcl->bclhp', ch, states, jnp.exp(a_cum))
    return (y_diag + y_off).reshape(b, l, h, p), final


def ssd_mixer(z, xbc, dt_raw, lp, h0):
    bsz, t, _ = z.shape
    xbc = jax.nn.silu(depthwise_conv(xbc, lp['ssd_conv_w'], lp['ssd_conv_b']))
    xs = xbc[..., :SSD_INNER].reshape(bsz, t, SSD_HEADS, SSD_HEADDIM)
    bm = xbc[..., SSD_INNER:SSD_INNER + SSD_GROUPS * SSD_DSTATE].reshape(bsz, t, SSD_GROUPS, SSD_DSTATE)
    cm = xbc[..., SSD_INNER + SSD_GROUPS * SSD_DSTATE:].reshape(bsz, t, SSD_GROUPS, SSD_DSTATE)
    dt = jax.nn.softplus(dt_raw.astype(jnp.float32) + lp['ssd_dt_bias'].astype(jnp.float32))
    a_neg = -jnp.exp(lp['ssd_a_log'].astype(jnp.float32))
    y_f, h_f = ssd_scan(xs, dt[:, :, 0], a_neg[0], bm, cm, h0[:, 0])
    y_b, h_b = ssd_scan(jnp.flip(xs, 1), jnp.flip(dt[:, :, 1], 1), a_neg[1], jnp.flip(bm, 1),
                        jnp.flip(cm, 1), h0[:, 1])
    y = y_f + jnp.flip(y_b, 1) + xs.astype(jnp.float32) * lp['ssd_d'].astype(jnp.float32)[:, None]
    y = y.reshape(bsz, t, SSD_INNER) * jax.nn.silu(z.astype(jnp.float32))
    y = rmsnorm(y, lp['ssd_norm_g'])
    return y.astype(z.dtype), jnp.stack([h_f, h_b], axis=1).astype(z.dtype)


def conv_module(u, lp):
    a, gt = jnp.split(u, 2, axis=-1)
    hcv = a * jax.nn.sigmoid(gt)
    hcv = depthwise_conv(hcv, lp['cm_dw_w'], lp['cm_dw_b'])
    hcv = layernorm(hcv, lp['cm_ln_g'], lp['cm_ln_b'])
    return jax.nn.silu(hcv)


def mixer(h, lp, rope, ctx):
    bsz, t, _ = h.shape
    proj = jnp.einsum('btd,df->btf', h, lp['w_in'])
    c_q, c_kv, k_pe, z, xbc, dt_raw, glu = jnp.split(proj, IN_OFFSETS, axis=-1)
    q = jnp.einsum('btr,rf->btf', rmsnorm(c_q, lp['q_norm_g']), lp['w_uq'])
    q = q.reshape(bsz, t, MLA_HEADS, QK_NOPE + QK_ROPE)
    q_nope, q_pe = q[..., :QK_NOPE], q[..., QK_NOPE:]
    c_kv = rmsnorm(c_kv, lp['kv_norm_g'])
    if rope is None:
        k_pe_used = k_pe
        h0 = jnp.zeros((bsz, 2, SSD_HEADS, SSD_HEADDIM, SSD_DSTATE), jnp.float32)
    else:
        cos, sin = rope
        q_pe = apply_axial_rope(q_pe, cos[:, None], sin[:, None])
        k_pe_used = apply_axial_rope(k_pe, cos, sin)
        h0 = ctx[2]
    k, v = mla_keys(c_kv, k_pe_used, lp['w_ukv'])
    if ctx is not None:
        k_c, v_c = mla_keys(ctx[0], ctx[1], lp['w_ukv'])
        k = jnp.concatenate([k_c.astype(k.dtype), k], axis=1)
        v = jnp.concatenate([v_c.astype(v.dtype), v], axis=1)
    attn = blocked_attention(jnp.concatenate([q_nope, q_pe], axis=-1), k, v, MLA_SCALE)
    attn = attn.reshape(bsz, t, MLA_OUT)
    y_ssd, h_fin = ssd_mixer(z, xbc, dt_raw.reshape(bsz, t, 2, SSD_HEADS), lp, h0)
    y_conv = conv_module(glu, lp)
    cat = jnp.concatenate([attn, y_ssd.astype(attn.dtype), y_conv.astype(attn.dtype)], axis=-1)
    out = jnp.einsum('btf,fd->btd', cat, lp['w_out'])
    ctx_out = (c_kv, k_pe, h_fin) if ctx is None else None
    return out, ctx_out


def peer_ffn(h, w_q, sub_keys, u_tab, v_tab):
    bsz, t, d = h.shape
    tok = h.reshape(-1, PEER_BLOCK, d)

    def one(hb):
        q = (hb @ w_q).reshape(PEER_BLOCK, PEER_HEADS, 2, PEER_QDIM // 2)
        s = jnp.einsum('thsk,snk->thsn', q, sub_keys).astype(jnp.float32)
        top_s, top_i = lax.top_k(s, PEER_TOPK)
        cand = top_s[..., 0, :, None] + top_s[..., 1, None, :]
        cand_idx = top_i[..., 0, :, None] * PEER_NKEYS + top_i[..., 1, None, :]
        best_s, best_j = lax.top_k(cand.reshape(PEER_BLOCK, PEER_HEADS, PEER_TOPK * PEER_TOPK), PEER_TOPK)
        idx = jnp.take_along_axis(cand_idx.reshape(PEER_BLOCK, PEER_HEADS, PEER_TOPK * PEER_TOPK), best_j, axis=-1)
        gate = jax.nn.softmax(best_s, axis=-1).astype(hb.dtype)
        act = jax.nn.gelu(jnp.einsum('thkd,td->thk', u_tab[idx], hb), approximate=False)
        return jnp.einsum('thk,thkd->td', gate * act, v_tab[idx])

    return lax.map(one, tok).reshape(bsz, t, d)


def trunk_layer(x, cond, lp, rope, ctx):
    m = jnp.einsum('bsd,df->bsf', jax.nn.silu(cond), lp['w_ada']) + lp['b_ada']
    sh1, sc1, g1, sh2, sc2, g2 = jnp.split(m, 6, axis=-1)
    hmix = rmsnorm(x, lp['norm1_g']) * (1 + sc1) + sh1
    mix, ctx_out = mixer(hmix, lp, rope, ctx)
    x = x + g1 * mix
    hff = rmsnorm(x, lp['norm2_g']) * (1 + sc2) + sh2
    x = x + g2 * peer_ffn(hff, lp['peer_wq'], lp['peer_keys'], lp['peer_u'], lp['peer_v'])
    return x, ctx_out


def setup_inputs(seed: int = 0) -> dict:
    key = jax.random.key(seed)
    ks = jax.random.split(key, 40)

    def nrm(i, shape, scale):
        return jax.random.normal(ks[i], shape, jnp.float32) * scale

    L, D = DEPTH, D_MODEL
    dt0 = jnp.exp(jax.random.uniform(ks[30], (L, 2, SSD_HEADS), jnp.float32, math.log(1e-3), math.log(1e-1)))
    return {
        'x_prompt': nrm(0, (BATCH, SEQ, D), 1.0),
        'x_sample': nrm(1, (DEC_BATCH, DEC_SEQ, D), 1.0),
        'cache_ckv': nrm(2, (DEC_BATCH, DEPTH, PAST_LEN, KV_LORA), 1.0),
        'cache_kpe': nrm(3, (DEC_BATCH, DEPTH, PAST_LEN, QK_ROPE), 1.0),
        'state_ssm': nrm(4, (DEC_BATCH, DEPTH, 2, SSD_HEADS, SSD_HEADDIM, SSD_DSTATE), 0.5),
        'c': nrm(5, (DEC_BATCH, D), 1.0),
        'c_ctx': nrm(6, (D,), 1.0),
        'norm1_g': 1.0 + nrm(7, (L, D), 0.02),
        'norm2_g': 1.0 + nrm(8, (L, D), 0.02),
        'w_ada': nrm(9, (L, D, 6 * D), 0.5 * D ** -0.5),
        'b_ada': nrm(10, (L, 6 * D), 0.02),
        'w_in': nrm(11, (L, D, IN_DIM), D ** -0.5),
        'q_norm_g': 1.0 + nrm(12, (L, Q_LORA), 0.02),
        'w_uq': nrm(13, (L, Q_LORA, MLA_HEADS * (QK_NOPE + QK_ROPE)), Q_LORA ** -0.5),
        'kv_norm_g': 1.0 + nrm(14, (L, KV_LORA), 0.02),
        'w_ukv': nrm(15, (L, KV_LORA, MLA_HEADS * (QK_NOPE + V_HEAD)), KV_LORA ** -0.5),
        'ssd_conv_w': nrm(16, (L, SSD_CONV, SSD_CONV_DIM), SSD_CONV ** -0.5),
        'ssd_conv_b': nrm(17, (L, SSD_CONV_DIM), 0.02),
        'ssd_dt_bias': dt0 + jnp.log(-jnp.expm1(-dt0)),
        'ssd_a_log': jnp.log(jax.random.uniform(ks[18], (L, 2, SSD_HEADS), jnp.float32, 1.0, 16.0)),
        'ssd_d': 1.0 + nrm(19, (L, SSD_HEADS), 0.1),
        'ssd_norm_g': 1.0 + nrm(20, (L, SSD_INNER), 0.02),
        'cm_dw_w': nrm(21, (L, CM_KERNEL, CM_CH), CM_KERNEL ** -0.5),
        'cm_dw_b': nrm(22, (L, CM_CH), 0.02),
        'cm_ln_g': 1.0 + nrm(23, (L, CM_CH), 0.02),
        'cm_ln_b': nrm(24, (L, CM_CH), 0.02),
        'w_out': nrm(25, (L, D_MIX, D), D_MIX ** -0.5),
        'peer_wq': nrm(26, (L, D, PEER_HEADS * PEER_QDIM), D ** -0.5),
        'peer_keys': nrm(27, (L, 2, PEER_NKEYS, PEER_QDIM // 2), (PEER_QDIM // 2) ** -0.5),
        'peer_u': nrm(28, (L, PEER_EXPERTS, D), D ** -0.5),
        'peer_v': nrm(29, (L, PEER_EXPERTS, D), PEER_HEADS ** -0.5),
        'final_g': 1.0 + nrm(31, (D,), 0.02),
    }


def reference(x_prompt, x_sample, cache_ckv, cache_kpe, state_ssm, c, c_ctx,
              norm1_g, norm2_g, w_ada, b_ada, w_in, q_norm_g, w_uq, kv_norm_g, w_ukv,
              ssd_conv_w, ssd_conv_b, ssd_dt_bias, ssd_a_log, ssd_d, ssd_norm_g,
              cm_dw_w, cm_dw_b, cm_ln_g, cm_ln_b, w_out,
              peer_wq, peer_keys, peer_u, peer_v, final_g):
    rope = axial_rope_tables(x_sample.shape[1])
    cond_ctx = c_ctx[None, None, :]
    cond_lat = c[:, None, :]
    yp, ys = x_prompt, x_sample
    ckv_list, kpe_list, ssm_list = [], [], []
    for l in range(DEPTH):
        lp = {
            'norm1_g': norm1_g[l], 'norm2_g': norm2_g[l], 'w_ada': w_ada[l], 'b_ada': b_ada[l],
            'w_in': w_in[l], 'q_norm_g': q_norm_g[l], 'w_uq': w_uq[l], 'kv_norm_g': kv_norm_g[l],
            'w_ukv': w_ukv[l], 'ssd_conv_w': ssd_conv_w[l], 'ssd_conv_b': ssd_conv_b[l],
            'ssd_dt_bias': ssd_dt_bias[l], 'ssd_a_log': ssd_a_log[l], 'ssd_d': ssd_d[l],
            'ssd_norm_g': ssd_norm_g[l], 'cm_dw_w': cm_dw_w[l], 'cm_dw_b': cm_dw_b[l],
            'cm_ln_g': cm_ln_g[l], 'cm_ln_b': cm_ln_b[l], 'w_out': w_out[l],
            'peer_wq': peer_wq[l], 'peer_keys': peer_keys[l], 'peer_u': peer_u[l], 'peer_v': peer_v[l],
        }
        yp, (ckv_l, kpe_l, ssm_l) = trunk_layer(yp, cond_ctx, lp, None, None)
        ckv_list.append(ckv_l)
        kpe_list.append(kpe_l)
        ssm_list.append(ssm_l)
        ys, _ = trunk_layer(ys, cond_lat, lp, rope, (cache_ckv[:, l], cache_kpe[:, l], state_ssm[:, l]))
    y_prompt = rmsnorm(yp, final_g)
    y_sample = rmsnorm(ys, final_g)
    new_ckv = jnp.stack(ckv_list, axis=1)
    new_kpe = jnp.stack(kpe_list, axis=1)
    new_ssm = jnp.stack(ssm_list, axis=1)
    return (y_prompt, y_sample, new_ckv, new_kpe, new_ssm)
```

```python
import functools
import math

import jax
import jax.numpy as jnp
from jax import lax
from jax.experimental import pallas as pl
from jax.experimental.pallas import tpu as pltpu

F32 = jnp.float32
BF16 = jnp.bfloat16

D_MODEL = 1024
DEPTH = 2
GRID_W = 64
EPS = 1e-6
MLA_HEADS = 4
Q_LORA = 256
KV_LORA = 128
QK_NOPE = 64
QK_ROPE = 32
V_HEAD = 64
ROPE_THETA = 10000.0
ROPE_FREQS = QK_ROPE // 4
MLA_SCALE = (QK_NOPE + QK_ROPE) ** -0.5
SSD_HEADS = 8
SSD_HEADDIM = 64
SSD_INNER = SSD_HEADS * SSD_HEADDIM
SSD_GROUPS = 2
SSD_DSTATE = 64
SSD_CONV = 5
SSD_CHUNK = 128
SSD_CONV_DIM = SSD_INNER + 2 * SSD_GROUPS * SSD_DSTATE
CM_CH = 256
CM_KERNEL = 31
PEER_HEADS = 8
PEER_NKEYS = 128
PEER_QDIM = 256
PEER_TOPK = 16
PEER_PICKS = PEER_HEADS * PEER_TOPK

LANES = 128
HALO = 16

COL_CQ = 0
COL_CKV = COL_CQ + Q_LORA
COL_Z = COL_CKV + KV_LORA
COL_XBC = COL_Z + SSD_INNER
COL_GLU = COL_XBC + SSD_CONV_DIM
COL_KPE = COL_GLU + 2 * CM_CH
COL_KPE_ROT = COL_KPE + LANES
COL_DT = COL_KPE_ROT + LANES
N_IN = COL_DT + LANES
HEAD_W = LANES

VMEM_LIMIT_BYTES = 56 * 1024 * 1024

TOKEN_TILE = 256
ATTN_Q_TILE = 256
TOPK_TILE = 128
EXPERT_TILE = 128
EXPERT_SUB = 8
EXPERT_SLOTS = 3


def _params(*sem):
    return pltpu.CompilerParams(dimension_semantics=sem, vmem_limit_bytes=VMEM_LIMIT_BYTES)


def _rms(x, g):
    return x * lax.rsqrt(jnp.mean(x * x, axis=-1, keepdims=True) + EPS) * g


def _silu(x):
    return x * jax.nn.sigmoid(x)


def _dot(a, b):
    return jnp.dot(a, b, preferred_element_type=F32)


def _dot_nt(a, b):
    return lax.dot_general(a, b, (((1,), (1,)), ((), ())), preferred_element_type=F32)


def _split3(v):
    hi = v.astype(BF16)
    r = v - hi.astype(F32)
    mid = r.astype(BF16)
    lo = (r - mid.astype(F32)).astype(BF16)
    return hi, mid, lo


def _dot_exact_l(m01, v):
    hi, mid, lo = _split3(v)
    return _dot(m01, hi) + _dot(m01, mid) + _dot(m01, lo)


def _dot_exact_r(v, m01):
    hi, mid, lo = _split3(v)
    return _dot(hi, m01) + _dot(mid, m01) + _dot(lo, m01)


def _ada_kernel(c_ref, w_ref, b_ref, o_ref):
    c = c_ref[...]
    o_ref[0] = _dot(_silu(c).astype(BF16), w_ref[0].astype(BF16)) + b_ref[0]


def _ada(cond8, w_ada, b_ada):
    nl, d, n = w_ada.shape
    tn = 1024
    return pl.pallas_call(
        _ada_kernel,
        grid=(nl, n // tn),
        in_specs=[pl.BlockSpec((8, d), lambda l, j: (0, 0)),
                  pl.BlockSpec((1, d, tn), lambda l, j: (l, 0, j)),
                  pl.BlockSpec((1, 1, tn), lambda l, j: (l, 0, j))],
        out_specs=pl.BlockSpec((1, 8, tn), lambda l, j: (l, 0, j)),
        out_shape=jax.ShapeDtypeStruct((nl, 8, n), F32),
        compiler_params=_params("parallel", "parallel"),
        name="ada",
    )(cond8, w_ada, b_ada.reshape(nl, 1, n))


def _inproj_kernel(*refs, rope):
    if rope:
        (x_ref, mod_ref, n1g_ref, win_ref, qg_ref, wqa_ref, wqb_ref, kvg_ref, cq_ref, sq_ref, ck_ref, sk_ref,
         q_out, ckv_out, kpe_out, kper_out, z_out, xbc_out, glu_out, dt_out) = refs
    else:
        (x_ref, mod_ref, n1g_ref, win_ref, qg_ref, wqa_ref, kvg_ref,
         q_out, ckv_out, kpe_out, kper_out, z_out, xbc_out, glu_out, dt_out) = refs
    x = x_ref[0]
    h = _rms(x, n1g_ref[...]) * (1.0 + mod_ref[0, 1:2, :]) + mod_ref[0, 0:1, :]
    proj = _dot(h.astype(BF16), win_ref[...])
    cqn = _rms(proj[:, COL_CQ:COL_CQ + Q_LORA], qg_ref[...]).astype(BF16)
    q = _dot(cqn, wqa_ref[...])
    kpe = proj[:, COL_KPE:COL_KPE + LANES]
    if rope:
        cq = jnp.concatenate([cq_ref[...]] * MLA_HEADS, axis=-1)
        sq = jnp.concatenate([sq_ref[...]] * MLA_HEADS, axis=-1)
        q = q * cq + _dot(cqn, wqb_ref[...]) * sq
        kper = kpe * ck_ref[...] + proj[:, COL_KPE_ROT:COL_KPE_ROT + LANES] * sk_ref[...]
    else:
        kper = kpe
    q_out[0] = (q * MLA_SCALE).astype(BF16)
    ckv_out[0] = _rms(proj[:, COL_CKV:COL_CKV + KV_LORA], kvg_ref[...])
    kpe_out[0] = kpe
    kper_out[0] = kper
    z_out[0] = proj[:, COL_Z:COL_Z + SSD_INNER]
    xbc_out[0] = proj[:, COL_XBC:COL_XBC + SSD_CONV_DIM]
    glu_out[0] = proj[:, COL_GLU:COL_GLU + 2 * CM_CH]
    dt_out[0] = proj[:, COL_DT:COL_DT + LANES]


def _inproj(x, mod, per_batch, lw, rope_tabs):
    b, s, d = x.shape
    tm = TOKEN_TILE
    rope = rope_tabs is not None
    qw = MLA_HEADS * HEAD_W
    full = lambda shape: pl.BlockSpec(shape, lambda bi, i: (0,) * len(shape))
    tok = lambda w: pl.BlockSpec((1, tm, w), lambda bi, i: (bi, i, 0))
    mod_spec = pl.BlockSpec((1, 6, d), (lambda bi, i: (bi, 0, 0)) if per_batch else (lambda bi, i: (0, 0, 0)))
    in_specs = [tok(d), mod_spec, full((1, d)), full((d, N_IN)), full((1, Q_LORA)), full((Q_LORA, qw))]
    args = [x, mod, lw["norm1_g"], lw["w_in"], lw["q_norm_g"], lw["w_uq_a"]]
    if rope:
        in_specs.append(full((Q_LORA, qw)))
        args.append(lw["w_uq_b"])
    in_specs.append(full((1, KV_LORA)))
    args.append(lw["kv_norm_g"])
    if rope:
        in_specs += [pl.BlockSpec((tm, LANES), lambda bi, i: (i, 0))] * 4
        args += list(rope_tabs)
    widths = [(qw, BF16), (KV_LORA, F32), (LANES, F32), (LANES, F32), (SSD_INNER, F32), (SSD_CONV_DIM, F32),
              (2 * CM_CH, F32), (LANES, F32)]
    return pl.pallas_call(
        functools.partial(_inproj_kernel, rope=rope),
        grid=(b, s // tm),
        in_specs=in_specs,
        out_specs=[tok(w) for w, _ in widths],
        out_shape=[jax.ShapeDtypeStruct((b, s, w), dt) for w, dt in widths],
        compiler_params=_params("parallel", "parallel"),
        name="inproj",
    )(*args)


def _kv_kernel(ckv_ref, kper_ref, wkt_ref, ekt_ref, wv_ref, kt_out, v_out):
    c = ckv_ref[0].astype(BF16)
    kp = kper_ref[0].astype(BF16)
    kt_out[0] = (_dot_nt(wkt_ref[...], c) + _dot_nt(ekt_ref[...], kp)).astype(BF16)
    v_out[0] = _dot(c, wv_ref[...]).astype(BF16)


def _kvbuild(ckvn, kper, lw):
    b, s, _ = ckvn.shape
    tm = TOKEN_TILE
    qw = MLA_HEADS * HEAD_W
    vw = MLA_HEADS * V_HEAD
    full = lambda shape: pl.BlockSpec(shape, lambda bi, i: (0,) * len(shape))
    return pl.pallas_call(
        _kv_kernel,
        grid=(b, s // tm),
        in_specs=[pl.BlockSpec((1, tm, KV_LORA), lambda bi, i: (bi, i, 0)),
                  pl.BlockSpec((1, tm, LANES), lambda bi, i: (bi, i, 0)),
                  full((qw, KV_LORA)), full((qw, LANES)), full((KV_LORA, vw))],
        out_specs=[pl.BlockSpec((1, qw, tm), lambda bi, i: (bi, 0, i)),
                   pl.BlockSpec((1, tm, vw), lambda bi, i: (bi, i, 0))],
        out_shape=[jax.ShapeDtypeStruct((b, qw, s), BF16), jax.ShapeDtypeStruct((b, s, vw), BF16)],
        compiler_params=_params("parallel", "parallel"),
        name="kvbuild",
    )(ckvn, kper, lw["w_k_t"], lw["e_k_t"], lw["w_v"])


def _attn_kernel(q_ref, kt_ref, v_ref, o_ref):
    tq = q_ref.shape[1]
    low = lax.broadcasted_iota(jnp.int32, (tq, LANES), 1) < V_HEAD
    for pair in range(MLA_HEADS // 2):
        vp = v_ref[0, :, pair * LANES:(pair + 1) * LANES]
        outs = []
        for hh in range(2):
            hd = 2 * pair + hh
            s = _dot(q_ref[0, :, hd * HEAD_W:(hd + 1) * HEAD_W], kt_ref[0, hd * HEAD_W:(hd + 1) * HEAD_W, :])
            e = jnp.exp(s - jnp.max(s, axis=-1, keepdims=True))
            den = jnp.sum(e, axis=-1, keepdims=True)
            outs.append(_dot(e.astype(BF16), vp) / den)
        o_ref[0, :, pair * LANES:(pair + 1) * LANES] = jnp.where(low, outs[0], outs[1]).astype(o_ref.dtype)


def _attention(q, kt, v):
    b, s, qw = q.shape
    tk = kt.shape[2]
    vw = v.shape[2]
    tq = ATTN_Q_TILE
    return pl.pallas_call(
        _attn_kernel,
        grid=(b, s // tq),
        in_specs=[pl.BlockSpec((1, tq, qw), lambda bi, i: (bi, i, 0)),
                  pl.BlockSpec((1, qw, tk), lambda bi, i: (bi, 0, 0)),
                  pl.BlockSpec((1, tk, vw), lambda bi, i: (bi, 0, 0))],
        out_specs=pl.BlockSpec((1, tq, vw), lambda bi, i: (bi, i, 0)),
        out_shape=jax.ShapeDtypeStruct((b, s, vw), BF16),
        compiler_params=_params("parallel", "parallel"),
        name="attn",
    )(q, kt, v)


def _seqpre_kernel(xc, xp, xn, gc, gp, gn, dt_ref, cw, cb, dw, db, lng, lnb, dtb,
                   xbc_out, yconv_out, dt_out, xpad, gpad):
    ts = xc.shape[1]
    i = pl.program_id(1)
    first = i == 0
    last = i == pl.num_programs(1) - 1

    def glu(val):
        return val[:, :CM_CH] * jax.nn.sigmoid(val[:, CM_CH:])

    xpad[0:HALO, :] = jnp.where(first, 0.0, xp[0])
    xpad[HALO:HALO + ts, :] = xc[0]
    xpad[HALO + ts:HALO + ts + HALO, :] = jnp.where(last, 0.0, xn[0])
    gpad[0:HALO, :] = jnp.where(first, 0.0, glu(gp[0]))
    gpad[HALO:HALO + ts, :] = glu(gc[0])
    gpad[HALO + ts:HALO + ts + HALO, :] = jnp.where(last, 0.0, glu(gn[0]))

    acc = jnp.broadcast_to(cb[...], (ts, SSD_CONV_DIM))
    for j in range(SSD_CONV):
        acc = acc + xpad[pl.ds(HALO - SSD_CONV // 2 + j, ts), :] * cw[j:j + 1, :]
    xbc_out[0] = _silu(acc)

    acc = jnp.broadcast_to(db[...], (ts, CM_CH))
    for j in range(CM_KERNEL):
        acc = acc + gpad[pl.ds(HALO - CM_KERNEL // 2 + j, ts), :] * dw[j:j + 1, :]
    mu = jnp.mean(acc, axis=-1, keepdims=True)
    cen = acc - mu
    y = cen * lax.rsqrt(jnp.mean(cen * cen, axis=-1, keepdims=True) + EPS) * lng[...] + lnb[...]
    yconv_out[0] = _silu(y).astype(BF16)

    t = dt_ref[0] + dtb[...]
    dt_out[0] = jnp.maximum(t, 0.0) + jnp.log1p(jnp.exp(-jnp.abs(t)))


def _seqpre(xbc, glu, dtraw, lw):
    b, s, _ = xbc.shape
    ts = min(s, 512)
    nh = ts // HALO
    last_h = s // HALO - 1
    cur = lambda w: pl.BlockSpec((1, ts, w), lambda bi, i: (bi, i, 0))
    prev = lambda w: pl.BlockSpec((1, HALO, w), lambda bi, i: (bi, jnp.maximum(i * nh - 1, 0), 0))
    nxt = lambda w: pl.BlockSpec((1, HALO, w), lambda bi, i: (bi, jnp.minimum((i + 1) * nh, last_h), 0))
    full = lambda shape: pl.BlockSpec(shape, lambda bi, i: (0,) * len(shape))
    xw, gw = SSD_CONV_DIM, 2 * CM_CH
    return pl.pallas_call(
        _seqpre_kernel,
        grid=(b, s // ts),
        in_specs=[cur(xw), prev(xw), nxt(xw), cur(gw), prev(gw), nxt(gw), cur(LANES),
                  full((SSD_CONV, xw)), full((1, xw)), full((CM_KERNEL, CM_CH)), full((1, CM_CH)),
                  full((1, CM_CH)), full((1, CM_CH)), full((1, LANES))],
        out_specs=[cur(xw), cur(CM_CH), cur(LANES)],
        out_shape=[jax.ShapeDtypeStruct((b, s, xw), F32), jax.ShapeDtypeStruct((b, s, CM_CH), BF16),
                   jax.ShapeDtypeStruct((b, s, LANES), F32)],
        scratch_shapes=[pltpu.VMEM((ts + 2 * HALO, xw), F32), pltpu.VMEM((ts + 2 * HALO, CM_CH), F32)],
        compiler_params=_params("parallel", "parallel"),
        name="seqpre",
    )(xbc, xbc, xbc, glu, glu, glu, dtraw, lw["ssd_conv_w"], lw["ssd_conv_b"], lw["cm_dw_w"], lw["cm_dw_b"],
      lw["cm_ln_g"], lw["cm_ln_b"], lw["dt_bias"])


def _ssd_kernel(xbc_ref, dt_ref, alog_ref, exp_ref, h0_ref, y_out, hfin_out, st, *, reverse):
    lc = SSD_CHUNK
    c = pl.program_id(1)

    @pl.when(c == 0)
    def _():
        st[...] = h0_ref[0]

    xs = xbc_ref[0, :, 0:SSD_INNER]
    bm = xbc_ref[0, :, SSD_INNER:SSD_INNER + LANES]
    cm = xbc_ref[0, :, SSD_INNER + LANES:SSD_INNER + 2 * LANES]
    dt = dt_ref[0]
    a = dt * (-jnp.exp(alog_ref[...]))

    row = lax.broadcasted_iota(jnp.int32, (lc, lc), 0)
    col = lax.broadcasted_iota(jnp.int32, (lc, lc), 1)
    keep = (col >= row) if reverse else (col <= row)
    cs = _dot_exact_l(keep.astype(BF16), a)
    cs_t = cs.T
    end = cs[0:1, :] if reverse else cs[lc - 1:lc, :]
    expand = exp_ref[...]
    dt_x = _dot_exact_r(dt, expand)
    dec_x = _dot_exact_r(jnp.exp(end - cs), expand)
    ecs_x = _dot_exact_r(jnp.exp(cs), expand)
    eend_x = ecs_x[0:1, :] if reverse else ecs_x[lc - 1:lc, :]

    xdt = xs * dt_x
    xdt_b = xdt.astype(BF16)
    state = st[...]
    y_off = _dot(cm.astype(BF16), state.astype(BF16)) * ecs_x

    lane = lax.broadcasted_iota(jnp.int32, (lc, LANES), 1)
    low = lane < SSD_HEADDIM
    bm_t = bm.T.astype(BF16)
    rep = SSD_HEADS // SSD_GROUPS
    gn = SSD_DSTATE
    cb = [_dot(jnp.where((lane >= g * gn) & (lane < (g + 1) * gn), cm, 0.0).astype(BF16), bm_t)
          for g in range(SSD_GROUPS)]
    k0 = SSD_HEADS if reverse else 0
    for pair in range(SSD_HEADS // 2):
        xp = xdt_b[:, pair * LANES:(pair + 1) * LANES]
        outs = []
        for hh in range(2):
            hd = 2 * pair + hh
            k = k0 + hd
            lmat = jnp.where(keep, jnp.exp(cs[:, k:k + 1] - cs_t[k:k + 1, :]), 0.0)
            outs.append(_dot((cb[hd // rep] * lmat).astype(BF16), xp))
        y_out[0, :, pair * LANES:(pair + 1) * LANES] = (
            jnp.where(low, outs[0], outs[1]) + y_off[:, pair * LANES:(pair + 1) * LANES])

    srow = lax.broadcasted_iota(jnp.int32, (SSD_GROUPS * gn, SSD_INNER), 0) // gn
    scol = lax.broadcasted_iota(jnp.int32, (SSD_GROUPS * gn, SSD_INNER), 1) // (rep * SSD_HEADDIM)
    new = _dot(bm_t, (xdt * dec_x).astype(BF16))
    state = state * eend_x + jnp.where(srow == scol, new, 0.0)
    st[...] = state
    hfin_out[0] = state


def _ssd(xbc_c, dt, lw, h0, reverse):
    b, s, _ = xbc_c.shape
    lc = SSD_CHUNK
    nc = s // lc
    chunk = (lambda bi, c: (bi, nc - 1 - c, 0)) if reverse else (lambda bi, c: (bi, c, 0))
    sn = SSD_GROUPS * SSD_DSTATE
    return pl.pallas_call(
        functools.partial(_ssd_kernel, reverse=reverse),
        grid=(b, nc),
        in_specs=[pl.BlockSpec((1, lc, SSD_CONV_DIM), chunk),
                  pl.BlockSpec((1, lc, LANES), chunk),
                  pl.BlockSpec((1, LANES), lambda bi, c: (0, 0)),
                  pl.BlockSpec((LANES, SSD_INNER), lambda bi, c: (0, 0)),
                  pl.BlockSpec((1, sn, SSD_INNER), lambda bi, c: (bi, 0, 0))],
        out_specs=[pl.BlockSpec((1, lc, SSD_INNER), chunk),
                   pl.BlockSpec((1, sn, SSD_INNER), lambda bi, c: (bi, 0, 0))],
        out_shape=[jax.ShapeDtypeStruct((b, s, SSD_INNER), F32), jax.ShapeDtypeStruct((b, sn, SSD_INNER), F32)],
        scratch_shapes=[pltpu.VMEM((sn, SSD_INNER), F32)],
        compiler_params=_params("parallel", "arbitrary"),
        name="ssd_bwd" if reverse else "ssd_fwd",
    )(xbc_c, dt, lw["a_log"], lw["expand_b"] if reverse else lw["expand_f"], h0)


def _mixpost_kernel(attn_ref, yf_ref, yb_ref, xbc_ref, z_ref, yc_ref, x_ref, mod_ref, dsk_ref, sng_ref, wo_ref,
                    n2g_ref, wq_ref, keys_ref, x1_out, hff_out, sc_out):
    y = yf_ref[0] + yb_ref[0] + xbc_ref[0, :, 0:SSD_INNER] * dsk_ref[...]
    y = _rms(y * _silu(z_ref[0]), sng_ref[...]).astype(BF16)
    a_w = MLA_HEADS * V_HEAD
    mix = (_dot(attn_ref[0], wo_ref[0:a_w, :]) + _dot(y, wo_ref[a_w:a_w + SSD_INNER, :])
           + _dot(yc_ref[0], wo_ref[a_w + SSD_INNER:, :]))
    x1 = x_ref[0] + mod_ref[0, 2:3, :] * mix
    x1_out[0] = x1
    hff = _rms(x1, n2g_ref[...]) * (1.0 + mod_ref[0, 4:5, :]) + mod_ref[0, 3:4, :]
    hff_out[0] = hff
    q = _dot(hff.astype(BF16), wq_ref[...])
    half = PEER_QDIM // 2
    for hs in range(2 * PEER_HEADS):
        sc_out[0, hs] = _dot_nt(keys_ref[hs % 2], q[:, hs * half:(hs + 1) * half].astype(BF16))


def _mixpost(attn, yf, yb, xbc_c, z, yconv, x, mod, per_batch, lw):
    b, s, d = x.shape
    tm = TOKEN_TILE
    tok = lambda w: pl.BlockSpec((1, tm, w), lambda bi, i: (bi, i, 0))
    full = lambda shape: pl.BlockSpec(shape, lambda bi, i: (0,) * len(shape))
    mod_spec = pl.BlockSpec((1, 6, d), (lambda bi, i: (bi, 0, 0)) if per_batch else (lambda bi, i: (0, 0, 0)))
    nq = PEER_HEADS * PEER_QDIM
    nhs = 2 * PEER_HEADS
    return pl.pallas_call(
        _mixpost_kernel,
        grid=(b, s // tm),
        in_specs=[tok(MLA_HEADS * V_HEAD), tok(SSD_INNER), tok(SSD_INNER), tok(SSD_CONV_DIM), tok(SSD_INNER),
                  tok(CM_CH), tok(d), mod_spec, full((1, SSD_INNER)), full((1, SSD_INNER)), full((d, d)),
                  full((1, d)), full((d, nq)), full((2, PEER_NKEYS, PEER_QDIM // 2))],
        out_specs=[tok(d), tok(d), pl.BlockSpec((1, nhs, PEER_NKEYS, tm), lambda bi, i: (bi, 0, 0, i))],
        out_shape=[jax.ShapeDtypeStruct((b, s, d), F32), jax.ShapeDtypeStruct((b, s, d), F32),
                   jax.ShapeDtypeStruct((b, nhs, PEER_NKEYS, s), F32)],
        compiler_params=_params("parallel", "parallel"),
        name="mixpost",
    )(attn, yf, yb, xbc_c, z, yconv, x, mod, lw["ssd_d_x"], lw["ssd_norm_g"], lw["w_out"], lw["norm2_g"],
      lw["peer_wq"], lw["peer_keys"])


def _take_top(vals, payload, n_rows, k, s_ref, p_ref):
    rows = lax.broadcasted_iota(jnp.int32, vals.shape, 0)

    def body(r, cur):
        m = jnp.max(cur, axis=0, keepdims=True)
        am = jnp.min(jnp.where(cur == m, rows, n_rows), axis=0, keepdims=True)
        hit = rows == am
        s_ref[pl.ds(r, 1), :] = m
        if payload is None:
            p_ref[pl.ds(r, 1), :] = am
        else:
            p_ref[pl.ds(r, 1), :] = jnp.sum(jnp.where(hit, payload, 0), axis=0, keepdims=True)
        return jnp.where(hit, -jnp.inf, cur)

    lax.fori_loop(0, k, body, vals)


def _topk_kernel(sc_ref, idx_out, gate_out, s0, i0, s1, i1, cs, ci, bs, bi):
    k = PEER_TOPK

    def head(hd, carry):
        _take_top(sc_ref[0, 2 * hd], None, PEER_NKEYS, k, s0, i0)
        _take_top(sc_ref[0, 2 * hd + 1], None, PEER_NKEYS, k, s1, i1)
        for i in range(k):
            cs[i * k:(i + 1) * k, :] = s0[i:i + 1, :] + s1[...]
            ci[i * k:(i + 1) * k, :] = i0[i:i + 1, :] * PEER_NKEYS + i1[...]
        _take_top(cs[...], ci[...], k * k, k, bs, bi)
        best = bs[...]
        e = jnp.exp(best - jnp.max(best, axis=0, keepdims=True))
        off = pl.multiple_of(hd * k, k)
        gate_out[0, pl.ds(off, k), :] = e / jnp.sum(e, axis=0, keepdims=True)
        idx_out[0, pl.ds(off, k), :] = bi[...]
        return carry

    lax.fori_loop(0, PEER_HEADS, head, 0)


def _topk(sc):
    b, nhs, nk, s = sc.shape
    tk = TOPK_TILE
    k = PEER_TOPK
    out_spec = pl.BlockSpec((1, PEER_PICKS, tk), lambda bi, i: (bi, 0, i))
    return pl.pallas_call(
        _topk_kernel,
        grid=(b, s // tk),
        in_specs=[pl.BlockSpec((1, nhs, nk, tk), lambda bi, i: (bi, 0, 0, i))],
        out_specs=[out_spec, out_spec],
        out_shape=[jax.ShapeDtypeStruct((b, PEER_PICKS, s), jnp.int32),
                   jax.ShapeDtypeStruct((b, PEER_PICKS, s), F32)],
        scratch_shapes=[pltpu.VMEM((k, tk), F32), pltpu.VMEM((k, tk), jnp.int32),
                        pltpu.VMEM((k, tk), F32), pltpu.VMEM((k, tk), jnp.int32),
                        pltpu.VMEM((k * k, tk), F32), pltpu.VMEM((k * k, tk), jnp.int32),
                        pltpu.VMEM((k, tk), F32), pltpu.VMEM((k, tk), jnp.int32)],
        compiler_params=_params("parallel", "parallel"),
        name="topk",
    )(sc)


def _expert_kernel(idx_ref, gate_ref, hff_ref, x1_ref, mod_ref, fg_ref, tab_ref, out_ref, buf, sem, *, final):
    tt = hff_ref.shape[1]
    d = hff_ref.shape[2]
    nsub = tt // EXPERT_SUB
    rows_per_slot = EXPERT_SUB * PEER_PICKS
    nchunk = d // LANES

    def row_copy(row, slot, dst):
        return pltpu.make_async_copy(tab_ref.at[pl.ds(row, 1), :], buf.at[slot, pl.ds(dst, 1), :], sem.at[slot])

    def issue_token(sub, tloc, slot):
        t = sub * EXPERT_SUB + tloc

        def body(j, carry):
            row_copy(idx_ref[0, j, t], slot, tloc * PEER_PICKS + j).start()
            return carry

        lax.fori_loop(0, PEER_PICKS, body, 0, unroll=8)

    def wait_slot(slot):
        pltpu.make_async_copy(tab_ref.at[pl.ds(0, rows_per_slot), :], buf.at[slot], sem.at[slot]).wait()

    lane_t = lax.broadcasted_iota(jnp.int32, (PEER_PICKS, tt), 1)

    def compute_token(sub, tloc, slot):
        t = sub * EXPERT_SUB + tloc
        base = pl.multiple_of(tloc * PEER_PICKS, PEER_PICKS)
        gate = jnp.sum(jnp.where(lane_t == t, gate_ref[0], 0.0), axis=1, keepdims=True)
        hrow = hff_ref[0, pl.ds(t, 1), :]
        acc = jnp.zeros((PEER_PICKS, LANES), F32)
        for cidx in range(nchunk):
            lo = cidx * LANES
            acc = acc + buf[slot, pl.ds(base, PEER_PICKS), lo:lo + LANES] * hrow[:, lo:lo + LANES]
        sdot = jnp.sum(acc, axis=1, keepdims=True)
        w = gate * (0.5 * sdot * (1.0 + lax.erf(sdot * (2.0 ** -0.5))))
        parts = []
        for cidx in range(nchunk):
            lo = d + cidx * LANES
            parts.append(jnp.sum(w * buf[slot, pl.ds(base, PEER_PICKS), lo:lo + LANES], axis=0, keepdims=True))
        x2 = x1_ref[0, pl.ds(t, 1), :] + mod_ref[0, 5:6, :] * jnp.concatenate(parts, axis=1)
        if final:
            x2 = _rms(x2, fg_ref[...])
        out_ref[0, pl.ds(t, 1), :] = x2

    ahead = EXPERT_SLOTS - 1
    for sub in range(ahead):
        lax.fori_loop(0, EXPERT_SUB, lambda tloc, c, sub=sub: (issue_token(sub, tloc, sub), c)[1], 0)

    def sub_body(sub, carry):
        slot = sub % EXPERT_SLOTS
        nslot = (sub + ahead) % EXPERT_SLOTS
        wait_slot(slot)

        def tok_body(tloc, c):
            @pl.when(sub + ahead < nsub)
            def _():
                issue_token(sub + ahead, tloc, nslot)

            compute_token(sub, tloc, slot)
            return c

        lax.fori_loop(0, EXPERT_SUB, tok_body, 0)
        return carry

    lax.fori_loop(0, nsub, sub_body, 0)


def _expert(idx_t, gate_t, hff, x1, mod, per_batch, final_g, table, final):
    b, s, d = hff.shape
    tt = EXPERT_TILE
    tok = pl.BlockSpec((1, tt, d), lambda bi, i: (bi, i, 0))
    mod_spec = pl.BlockSpec((1, 6, d), (lambda bi, i: (bi, 0, 0)) if per_batch else (lambda bi, i: (0, 0, 0)))
    return pl.pallas_call(
        functools.partial(_expert_kernel, final=final),
        grid=(b, s // tt),
        in_specs=[pl.BlockSpec((1, PEER_PICKS, tt), lambda bi, i: (bi, 0, i), memory_space=pltpu.SMEM),
                  pl.BlockSpec((1, PEER_PICKS, tt), lambda bi, i: (bi, 0, i)),
                  tok, tok, mod_spec, pl.BlockSpec((1, d), lambda bi, i: (0, 0)),
                  pl.BlockSpec(memory_space=pl.ANY)],
        out_specs=tok,
        out_shape=jax.ShapeDtypeStruct((b, s, d), F32),
        scratch_shapes=[pltpu.VMEM((EXPERT_SLOTS, EXPERT_SUB * PEER_PICKS, 2 * d), F32),
                        pltpu.SemaphoreType.DMA((EXPERT_SLOTS,))],
        compiler_params=_params("arbitrary", "arbitrary"),
        name="expert",
    )(idx_t, gate_t, hff, x1, mod, final_g, table)


def _rope_tables(n_tokens):
    rows = n_tokens // GRID_W
    row = jnp.repeat(jnp.arange(rows, dtype=F32), GRID_W)
    col = jnp.tile(jnp.arange(GRID_W, dtype=F32), rows)
    inv = ROPE_THETA ** (-jnp.arange(ROPE_FREQS, dtype=F32) / ROPE_FREQS)
    ang = jnp.stack([row[:, None] * inv, col[:, None] * inv], axis=1)
    cos, sin = jnp.cos(ang), jnp.sin(ang)
    c32 = jnp.stack([cos, cos], axis=2).reshape(n_tokens, QK_ROPE)
    s32 = jnp.stack([-sin, sin], axis=2).reshape(n_tokens, QK_ROPE)
    t = n_tokens
    cq = jnp.concatenate([jnp.ones((t, QK_NOPE), F32), c32, jnp.zeros((t, HEAD_W - QK_NOPE - QK_ROPE), F32)], 1)
    sq = jnp.concatenate([jnp.zeros((t, QK_NOPE), F32), s32, jnp.zeros((t, HEAD_W - QK_NOPE - QK_ROPE), F32)], 1)
    ck = jnp.concatenate([c32, jnp.zeros((t, LANES - QK_ROPE), F32)], 1)
    sk = jnp.concatenate([s32, jnp.zeros((t, LANES - QK_ROPE), F32)], 1)
    return cq, sq, ck, sk


def _layer_weights(l, p):
    d = D_MODEL
    partner = jnp.arange(QK_ROPE) ^ ROPE_FREQS
    w_in = p["w_in"][l]
    o_q, o_kv = Q_LORA, Q_LORA + KV_LORA
    o_pe = o_kv + QK_ROPE
    o_z = o_pe + SSD_INNER
    o_x = o_z + SSD_CONV_DIM
    o_dt = o_x + 2 * SSD_HEADS
    kpe_w = w_in[:, o_kv:o_pe]
    pad = lambda w, n: jnp.pad(w, ((0, 0), (0, n - w.shape[1])))
    w_in_r = jnp.concatenate([w_in[:, :o_q], w_in[:, o_q:o_kv], w_in[:, o_pe:o_z], w_in[:, o_z:o_x], w_in[:, o_dt:],
                              pad(kpe_w, LANES), pad(kpe_w[:, partner], LANES), pad(w_in[:, o_x:o_dt], LANES)], axis=1)
    w_uq = p["w_uq"][l].reshape(Q_LORA, MLA_HEADS, QK_NOPE + QK_ROPE)
    zq = jnp.zeros((Q_LORA, MLA_HEADS, HEAD_W - QK_NOPE - QK_ROPE), F32)
    w_uq_a = jnp.concatenate([w_uq, zq], axis=2).reshape(Q_LORA, MLA_HEADS * HEAD_W)
    w_uq_b = jnp.concatenate([jnp.zeros((Q_LORA, MLA_HEADS, QK_NOPE), F32), w_uq[:, :, QK_NOPE:][:, :, partner], zq],
                             axis=2).reshape(Q_LORA, MLA_HEADS * HEAD_W)
    w_ukv = p["w_ukv"][l].reshape(KV_LORA, MLA_HEADS, QK_NOPE + V_HEAD)
    w_k = jnp.concatenate([w_ukv[:, :, :QK_NOPE], jnp.zeros((KV_LORA, MLA_HEADS, HEAD_W - QK_NOPE), F32)], axis=2)
    e_k = jnp.zeros((LANES, MLA_HEADS, HEAD_W), F32)
    e_k = e_k.at[jnp.arange(QK_ROPE), :, QK_NOPE + jnp.arange(QK_ROPE)].set(1.0)
    hl = jnp.arange(LANES)[:, None]
    cols = jnp.arange(SSD_INNER)[None, :] // SSD_HEADDIM
    row1 = lambda v: v.reshape(1, -1)
    dt_bias = jnp.pad(p["ssd_dt_bias"][l].reshape(-1), (0, LANES - 2 * SSD_HEADS))
    a_log = jnp.pad(p["ssd_a_log"][l].reshape(-1), (0, LANES - 2 * SSD_HEADS))
    return {
        "norm1_g": row1(p["norm1_g"][l]), "norm2_g": row1(p["norm2_g"][l]),
        "w_in": w_in_r.astype(BF16), "q_norm_g": row1(p["q_norm_g"][l]), "kv_norm_g": row1(p["kv_norm_g"][l]),
        "w_uq_a": w_uq_a.astype(BF16), "w_uq_b": w_uq_b.astype(BF16),
        "w_k_t": w_k.reshape(KV_LORA, MLA_HEADS * HEAD_W).T.astype(BF16),
        "e_k_t": e_k.reshape(LANES, MLA_HEADS * HEAD_W).T.astype(BF16),
        "w_v": w_ukv[:, :, QK_NOPE:].reshape(KV_LORA, MLA_HEADS * V_HEAD).astype(BF16),
        "ssd_conv_w": p["ssd_conv_w"][l], "ssd_conv_b": row1(p["ssd_conv_b"][l]),
        "cm_dw_w": p["cm_dw_w"][l], "cm_dw_b": row1(p["cm_dw_b"][l]),
        "cm_ln_g": row1(p["cm_ln_g"][l]), "cm_ln_b": row1(p["cm_ln_b"][l]),
        "dt_bias": row1(dt_bias), "a_log": row1(a_log),
        "expand_f": (hl == cols).astype(BF16), "expand_b": (hl == cols + SSD_HEADS).astype(BF16),
        "ssd_d_x": row1(jnp.repeat(p["ssd_d"][l], SSD_HEADDIM)), "ssd_norm_g": row1(p["ssd_norm_g"][l]),
        "w_out": p["w_out"][l].astype(BF16), "peer_wq": p["peer_wq"][l].astype(BF16),
        "peer_keys": p["peer_keys"][l].astype(BF16),
        "table": jnp.concatenate([p["peer_u"][l], p["peer_v"][l]], axis=1),
    }


def _state_in(h):
    b = h.shape[0]
    rep = SSD_HEADS // SSD_GROUPS
    ht = jnp.transpose(h.reshape(b, SSD_GROUPS, rep, SSD_HEADDIM, SSD_DSTATE), (0, 1, 4, 2, 3))
    eye = jnp.eye(SSD_GROUPS, dtype=h.dtype)
    full = ht[:, :, :, None] * eye[None, :, None, :, None, None]
    return full.reshape(b, SSD_GROUPS * SSD_DSTATE, SSD_INNER)


def _state_out(st):
    b = st.shape[0]
    rep = SSD_HEADS // SSD_GROUPS
    s6 = st.reshape(b, SSD_GROUPS, SSD_DSTATE, SSD_GROUPS, rep, SSD_HEADDIM)
    diag = jnp.stack([s6[:, g, :, g] for g in range(SSD_GROUPS)], axis=1)
    return jnp.transpose(diag, (0, 1, 3, 4, 2)).reshape(b, SSD_HEADS, SSD_HEADDIM, SSD_DSTATE)


def _trunk_layer(x, mod, per_batch, lw, rope_tabs, ctx, final_g, final):
    b = x.shape[0]
    q, ckvn, kpe, kper, z, xbc, glu, dtraw = _inproj(x, mod, per_batch, lw, rope_tabs)
    kt, v = _kvbuild(ckvn, kper, lw)
    if ctx is None:
        zero = jnp.zeros((b, SSD_HEADS, SSD_HEADDIM, SSD_DSTATE), F32)
        h0f, h0b = zero, zero
    else:
        c_ckv, c_kpe, c_ssm = ctx
        kt_c, v_c = _kvbuild(c_ckv, jnp.pad(c_kpe, ((0, 0), (0, 0), (0, LANES - QK_ROPE))), lw)
        kt = jnp.concatenate([kt_c, kt], axis=2)
        v = jnp.concatenate([v_c, v], axis=1)
        h0f, h0b = c_ssm[:, 0], c_ssm[:, 1]
    attn = _attention(q, kt, v)
    xbc_c, yconv, dt = _seqpre(xbc, glu, dtraw, lw)
    yf, hf = _ssd(xbc_c, dt, lw, _state_in(h0f), False)
    yb, hb = _ssd(xbc_c, dt, lw, _state_in(h0b), True)
    x1, hff, sc = _mixpost(attn, yf, yb, xbc_c, z, yconv, x, mod, per_batch, lw)
    idx_t, gate_t = _topk(sc)
    x2 = _expert(idx_t, gate_t, hff, x1, mod, per_batch, final_g, lw["table"], final)
    ssm = jnp.stack([_state_out(hf), _state_out(hb)], axis=1)
    return x2, (ckvn, kpe[:, :, :QK_ROPE], ssm)


def kernel(x_prompt, x_sample, cache_ckv, cache_kpe, state_ssm, c, c_ctx, norm1_g, norm2_g, w_ada, b_ada, w_in, q_norm_g, w_uq, kv_norm_g, w_ukv, ssd_conv_w, ssd_conv_b, ssd_dt_bias, ssd_a_log, ssd_d, ssd_norm_g, cm_dw_w, cm_dw_b, cm_ln_g, cm_ln_b, w_out, peer_wq, peer_keys, peer_u, peer_v, final_g):
    p = dict(norm1_g=norm1_g, norm2_g=norm2_g, w_in=w_in, q_norm_g=q_norm_g, w_uq=w_uq, kv_norm_g=kv_norm_g,
             w_ukv=w_ukv, ssd_conv_w=ssd_conv_w, ssd_conv_b=ssd_conv_b, ssd_dt_bias=ssd_dt_bias, ssd_a_log=ssd_a_log,
             ssd_d=ssd_d, ssd_norm_g=ssd_norm_g, cm_dw_w=cm_dw_w, cm_dw_b=cm_dw_b, cm_ln_g=cm_ln_g, cm_ln_b=cm_ln_b,
             w_out=w_out, peer_wq=peer_wq, peer_keys=peer_keys, peer_u=peer_u, peer_v=peer_v)
    d = D_MODEL
    nb = c.shape[0]
    cond = jnp.concatenate([c_ctx[None, :], c, jnp.zeros((8 - 1 - nb, d), F32)], axis=0)
    mods = _ada(cond, w_ada, b_ada).reshape(DEPTH, 8, 6, d)
    rope_tabs = _rope_tables(x_sample.shape[1])
    fg = final_g.reshape(1, d)
    yp, ys = x_prompt, x_sample
    ckv_list, kpe_list, ssm_list = [], [], []
    for l in range(DEPTH):
        lw = _layer_weights(l, p)
        final = l == DEPTH - 1
        yp, (ckv_l, kpe_l, ssm_l) = _trunk_layer(yp, mods[l, 0:1], False, lw, None, None, fg, final)
        ckv_list.append(ckv_l)
        kpe_list.append(kpe_l)
        ssm_list.append(ssm_l)
        ys, _ = _trunk_layer(ys, mods[l, 1:1 + nb], True, lw, rope_tabs,
                             (cache_ckv[:, l], cache_kpe[:, l], state_ssm[:, l]), fg, final)
    return (yp, ys, jnp.stack(ckv_list, axis=1), jnp.stack(kpe_list, axis=1), jnp.stack(ssm_list, axis=1))
```
